```python
import math
import jax, jax.numpy as jnp
from jax import lax
import numpy as np

D_MODEL = 1024
BATCH = 8
SEQ = 2048
DEPTH = 2
DEC_BATCH = 8
DEC_SEQ = 8192
PAST_LEN = 128

A_PATTERNS = ((128, 1), (512, 4), (2048, 16))
A_GROUPS = 3
A_HEADS = 8
A_HEAD_DIM = 64
A_QKV = A_GROUPS * A_HEADS * A_HEAD_DIM
A_OUT = A_HEADS * A_HEAD_DIM
LOCAL_BLK = 64
ROPE_THETA = 10000.0
B_WIDTH = 1024
B_GROUPS = 8
CHUNK = 128
N_BRANCHES = 2
C_IN = 3 * A_QKV + 2 * B_WIDTH + N_BRANCHES * D_MODEL
X_HEADS = 4
X_HEAD_DIM = D_MODEL // X_HEADS
X_WIDTH = X_HEADS * X_HEAD_DIM
N_MEM = 256
N_EXPERT_GROUPS = 4
EXPERTS_PER_GROUP = 8
N_EXPERTS = N_EXPERT_GROUPS * EXPERTS_PER_GROUP
TOP_K = 2
D_EXPERT = 512
ROW_BLK = 256
ALPHA = (2 * DEPTH) ** 0.25
BETA = (8 * DEPTH) ** -0.25
LN_EPS = 1e-5
NEG_INF = -1e30

kernel_name = "hybrid_dilated_gmlp_hmoe_encoder"


def layer_norm(x, g, b):
    xf = x.astype(jnp.float32)
    mu = jnp.mean(xf, axis=-1, keepdims=True)
    var = jnp.mean(jnp.square(xf - mu), axis=-1, keepdims=True)
    y = (xf - mu) * lax.rsqrt(var + LN_EPS)
    return (y * g.astype(jnp.float32) + b.astype(jnp.float32)).astype(x.dtype)


def rotary(t, pos):
    half = t.shape[-1] // 2
    inv = ROPE_THETA ** (-jnp.arange(half, dtype=jnp.float32) / half)
    ang = pos.astype(jnp.float32)[:, None] * inv[None, :]
    cos = jnp.cos(ang)[None, :, None, :]
    sin = jnp.sin(ang)[None, :, None, :]
    tf = t.astype(jnp.float32)
    t1, t2 = tf[..., :half], tf[..., half:]
    return jnp.concatenate([t1 * cos - t2 * sin, t1 * sin + t2 * cos], axis=-1).astype(t.dtype)


def stride_split(t, d):
    b, s = t.shape[:2]
    rest = t.shape[2:]
    t = t.reshape((b, s // d, d) + rest)
    return jnp.moveaxis(t, 2, 1).reshape((b * d, s // d) + rest)


def stride_merge(t, d, b):
    length = t.shape[1]
    rest = t.shape[2:]
    t = t.reshape((b, d, length) + rest)
    return jnp.moveaxis(t, 1, 2).reshape((b, d * length) + rest)


def banded_attention(q, k, v, radius):
    n, length, h, hd = q.shape
    nb = -(-length // LOCAL_BLK)
    lp = nb * LOCAL_BLK
    qb = jnp.pad(q, ((0, 0), (0, lp - length), (0, 0), (0, 0))).reshape(n, nb, LOCAL_BLK, h, hd)

    def neighbours(t):
        tp = jnp.pad(t, ((0, 0), (LOCAL_BLK, lp - length + LOCAL_BLK), (0, 0), (0, 0)))
        tp = tp.reshape(n, nb + 2, LOCAL_BLK, h, hd)
        return jnp.concatenate([tp[:, :-2], tp[:, 1:-1], tp[:, 2:]], axis=2)

    kb, vb = neighbours(k), neighbours(v)
    start = jnp.arange(nb)[:, None] * LOCAL_BLK
    qpos = start + jnp.arange(LOCAL_BLK)[None, :]
    kpos = start - LOCAL_BLK + jnp.arange(3 * LOCAL_BLK)[None, :]
    kp = kpos[:, None, :]
    valid = (jnp.abs(kp - qpos[:, :, None]) <= radius) & (kp >= 0) & (kp < length)
    s = jnp.einsum("nbqhd,nbkhd->nbhqk", qb, kb, preferred_element_type=jnp.float32) * (hd ** -0.5)
    s = jnp.where(valid[None, :, None], s, NEG_INF)
    m = jnp.max(s, axis=-1, keepdims=True)
    p = jnp.exp(s - m)
    den = jnp.sum(p, axis=-1, keepdims=True)
    o = jnp.einsum("nbhqk,nbkhd->nbqhd", p.astype(v.dtype), vb, preferred_element_type=jnp.float32)
    o = o / jnp.swapaxes(den, 2, 3)
    lse = jnp.swapaxes((m + jnp.log(den))[..., 0], 2, 3)
    o = o.reshape(n, lp, h, hd)[:, :length].astype(v.dtype)
    lse = lse.reshape(n, lp, h)[:, :length]
    return o, lse


def dilated_mixture(q, k, v):
    bsz, seq = q.shape[:2]
    outs, lses = [], []
    for g, (window, dil) in enumerate(A_PATTERNS):
        radius = window // (2 * dil)
        o, lse = banded_attention(stride_split(q[:, :, g], dil), stride_split(k[:, :, g], dil),
                                  stride_split(v[:, :, g], dil), radius)
        outs.append(stride_merge(o, dil, bsz))
        lses.append(stride_merge(lse, dil, bsz))
    wts = jax.nn.softmax(jnp.stack(lses, axis=0), axis=0)
    o = jnp.einsum("gbsh,gbshd->bshd", wts.astype(q.dtype), jnp.stack(outs, axis=0))
    return o.reshape(bsz, seq, A_OUT)


def spatial_gating(u, v, ln_g, ln_b, w_s, b_s):
    bsz, seq, width = u.shape
    vn = layer_norm(v, ln_g, ln_b).reshape(bsz, seq // CHUNK, CHUNK, B_GROUPS, width // B_GROUPS)
    vm = jnp.einsum("gpq,bcqgh->bcpgh", w_s, vn) + b_s.T[None, None, :, :, None]
    return u * vm.reshape(bsz, seq, width)


def memory_attention(x, mem, w_xq, w_xkv, w_xo):
    bsz, seq, _ = x.shape
    n_mem = mem.shape[1]
    q = (x @ w_xq).reshape(bsz, seq, X_HEADS, X_HEAD_DIM)
    kv = (mem @ w_xkv).reshape(bsz, n_mem, 2, X_HEADS, X_HEAD_DIM)
    s = jnp.einsum("bshd,bmhd->bhsm", q, kv[:, :, 0], preferred_element_type=jnp.float32) * (X_HEAD_DIM ** -0.5)
    p = jax.nn.softmax(s, axis=-1)
    o = jnp.einsum("bhsm,bmhd->bshd", p.astype(x.dtype), kv[:, :, 1]).reshape(bsz, seq, X_WIDTH)
    return o @ w_xo


def hierarchical_moe(x, w_rg, b_rg, w_re, b_re, w_gu, w_down):
    bsz, seq, d = x.shape
    xt = x.reshape(-1, d)
    t = xt.shape[0]
    g_logits = (xt @ w_rg + b_rg).astype(jnp.float32)
    g_sel = jnp.argmax(g_logits, axis=-1)
    p_group = jnp.take_along_axis(jax.nn.softmax(g_logits, axis=-1), g_sel[:, None], axis=1)[:, 0]
    e_logits = (xt @ w_re + b_re).astype(jnp.float32).reshape(t, N_EXPERT_GROUPS, EXPERTS_PER_GROUP)
    e_logits = jnp.take_along_axis(e_logits, g_sel[:, None, None], axis=1)[:, 0]
    top_v, top_i = lax.top_k(e_logits, TOP_K)
    w_tok = jax.nn.softmax(top_v, axis=-1) * p_group[:, None]
    expert = g_sel[:, None] * EXPERTS_PER_GROUP + top_i
    flat_e = expert.reshape(-1)
    flat_t = jnp.repeat(jnp.arange(t, dtype=jnp.int32), TOP_K)
    flat_w = w_tok.reshape(-1)
    n_assign = t * TOP_K
    order = jnp.argsort(flat_e)
    e_sorted, t_sorted, w_sorted = flat_e[order], flat_t[order], flat_w[order]
    counts = jnp.bincount(flat_e, length=N_EXPERTS)
    padded = ((counts + ROW_BLK - 1) // ROW_BLK) * ROW_BLK
    seg_start = jnp.cumsum(counts) - counts
    pad_end = jnp.cumsum(padded)
    pad_start = pad_end - padded
    dest = pad_start[e_sorted] + jnp.arange(n_assign) - seg_start[e_sorted]
    n_blocks = -(-n_assign // ROW_BLK) + N_EXPERTS
    rows = n_blocks * ROW_BLK
    row_tok = jnp.full((rows,), t, dtype=jnp.int32).at[dest].set(t_sorted)
    row_w = jnp.zeros((rows,), jnp.float32).at[dest].set(w_sorted)
    block_e = jnp.minimum(jnp.searchsorted(pad_end, jnp.arange(n_blocks) * ROW_BLK, side="right"),
                          N_EXPERTS - 1)
    x_pad = jnp.concatenate([xt, jnp.zeros((1, d), xt.dtype)], axis=0)

    def expert_block(args):
        tok, wt, e = args
        xb = x_pad[tok]
        gate, up = jnp.split(xb @ w_gu[e], 2, axis=-1)
        return ((jax.nn.silu(gate) * up) @ w_down[e]) * wt[:, None].astype(xb.dtype)

    yb = lax.map(expert_block, (row_tok.reshape(n_blocks, ROW_BLK), row_w.reshape(n_blocks, ROW_BLK), block_e))
    out = jax.ops.segment_sum(yb.reshape(rows, d), row_tok, num_segments=t + 1)[:t]
    return out.reshape(bsz, seq, d)


def encoder_trunk(x, mem, params):
    (w_in, b_in, gm_ln_g, gm_ln_b, w_sp, b_sp, w_a_out, w_b_out, w_o, ln1_g, ln1_b,
     w_xq, w_xkv, w_xo, ln2_g, ln2_b, w_rg, b_rg, w_re, b_re, w_gu, w_down, ln3_g, ln3_b) = params
    bsz, seq, _ = x.shape
    pos = jnp.arange(seq)
    splits = np.cumsum([A_QKV, A_QKV, A_QKV, B_WIDTH, B_WIDTH]).tolist()
    heads = (bsz, seq, A_GROUPS * A_HEADS, A_HEAD_DIM)
    gshape = (bsz, seq, A_GROUPS, A_HEADS, A_HEAD_DIM)
    for l in range(DEPTH):
        h = x @ w_in[l] + b_in[l]
        q, k, v, zu, zv, gate_logits = jnp.split(h, splits, axis=-1)
        q = rotary(q.reshape(heads), pos).reshape(gshape)
        k = rotary(k.reshape(heads), pos).reshape(gshape)
        a = dilated_mixture(q, k, v.reshape(gshape))
        bb = spatial_gating(jax.nn.gelu(zu), jax.nn.gelu(zv), gm_ln_g[l], gm_ln_b[l], w_sp[l], b_sp[l])
        gates = jax.nn.sigmoid(gate_logits).reshape(bsz, seq, N_BRANCHES, D_MODEL)
        merged = gates[:, :, 0] * (a @ w_a_out[l]) + gates[:, :, 1] * (bb @ w_b_out[l])
        x = layer_norm(ALPHA * x + merged @ w_o[l], ln1_g[l], ln1_b[l])
        x = layer_norm(ALPHA * x + memory_attention(x, mem, w_xq[l], w_xkv[l], w_xo[l]), ln2_g[l], ln2_b[l])
        x = layer_norm(ALPHA * x + hierarchical_moe(x, w_rg[l], b_rg[l], w_re[l], b_re[l], w_gu[l], w_down[l]),
                       ln3_g[l], ln3_b[l])
    return x


def setup_inputs(seed: int = 0) -> dict:
    key = jax.random.key(seed)
    ks = jax.random.split(key, 40)
    L = DEPTH

    def nrm(k, shape, scale):
        return jax.random.normal(k, shape, jnp.float32) * scale

    return {
        "x_prompt": nrm(ks[0], (BATCH, SEQ, D_MODEL), 1.0),
        "x_sample": nrm(ks[1], (DEC_BATCH, DEC_SEQ, D_MODEL), 1.0),
        "mem_prompt": nrm(ks[2], (BATCH, N_MEM, D_MODEL), 1.0),
        "mem_sample": nrm(ks[3], (DEC_BATCH, N_MEM, D_MODEL), 1.0),
        "w_in": nrm(ks[4], (L, D_MODEL, C_IN), D_MODEL ** -0.5),
        "b_in": nrm(ks[5], (L, C_IN), 0.02),
        "gm_ln_g": 1.0 + nrm(ks[6], (L, B_WIDTH), 0.02),
        "gm_ln_b": nrm(ks[7], (L, B_WIDTH), 0.02),
        "w_sp": nrm(ks[8], (L, B_GROUPS, CHUNK, CHUNK), CHUNK ** -0.5),
        "b_sp": 1.0 + nrm(ks[9], (L, B_GROUPS, CHUNK), 0.1),
        "w_a_out": nrm(ks[10], (L, A_OUT, D_MODEL), A_OUT ** -0.5),
        "w_b_out": nrm(ks[11], (L, B_WIDTH, D_MODEL), B_WIDTH ** -0.5),
        "w_o": nrm(ks[12], (L, D_MODEL, D_MODEL), BETA * D_MODEL ** -0.5),
        "ln1_g": 1.0 + nrm(ks[13], (L, D_MODEL), 0.02),
        "ln1_b": nrm(ks[14], (L, D_MODEL), 0.02),
        "w_xq": nrm(ks[15], (L, D_MODEL, X_WIDTH), D_MODEL ** -0.5),
        "w_xkv": nrm(ks[16], (L, D_MODEL, 2 * X_WIDTH), D_MODEL ** -0.5),
        "w_xo": nrm(ks[17], (L, X_WIDTH, D_MODEL), BETA * X_WIDTH ** -0.5),
        "ln2_g": 1.0 + nrm(ks[18], (L, D_MODEL), 0.02),
        "ln2_b": nrm(ks[19], (L, D_MODEL), 0.02),
        "w_rg": nrm(ks[20], (L, D_MODEL, N_EXPERT_GROUPS), D_MODEL ** -0.5),
        "b_rg": nrm(ks[21], (L, N_EXPERT_GROUPS), 0.01),
        "w_re": nrm(ks[22], (L, D_MODEL, N_EXPERTS), D_MODEL ** -0.5),
        "b_re": nrm(ks[23], (L, N_EXPERTS), 0.01),
        "w_gu": nrm(ks[24], (L, N_EXPERTS, D_MODEL, 2 * D_EXPERT), D_MODEL ** -0.5),
        "w_down": nrm(ks[25], (L, N_EXPERTS, D_EXPERT, D_MODEL), BETA * D_EXPERT ** -0.5),
        "ln3_g": 1.0 + nrm(ks[26], (L, D_MODEL), 0.02),
        "ln3_b": nrm(ks[27], (L, D_MODEL), 0.02),
    }


def reference(x_prompt, x_sample, mem_prompt, mem_sample, w_in, b_in, gm_ln_g, gm_ln_b, w_sp, b_sp,
              w_a_out, w_b_out, w_o, ln1_g, ln1_b, w_xq, w_xkv, w_xo, ln2_g, ln2_b,
              w_rg, b_rg, w_re, b_re, w_gu, w_down, ln3_g, ln3_b):
    params = (w_in, b_in, gm_ln_g, gm_ln_b, w_sp, b_sp, w_a_out, w_b_out, w_o, ln1_g, ln1_b,
              w_xq, w_xkv, w_xo, ln2_g, ln2_b, w_rg, b_rg, w_re, b_re, w_gu, w_down, ln3_g, ln3_b)
    y_prompt = encoder_trunk(x_prompt, mem_prompt, params)
    y_sample = encoder_trunk(x_sample, mem_sample, params)
    return (y_prompt, y_sample)
```

```python
import functools

import jax
import jax.numpy as jnp
from jax import lax
from jax.experimental import pallas as pl
from jax.experimental.pallas import tpu as pltpu

F32 = jnp.float32
BF16 = jnp.bfloat16

D_MODEL = 1024
DEPTH = 2
A_PATTERNS = ((128, 1), (512, 4), (2048, 16))
A_GROUPS = 3
A_HEADS = 8
A_HEAD_DIM = 64
A_QKV = A_GROUPS * A_HEADS * A_HEAD_DIM
A_OUT = A_HEADS * A_HEAD_DIM
RADIUS = 64
ROPE_THETA = 10000.0
B_WIDTH = 1024
B_GROUPS = 8
CHUNK = 128
C_IN = 3 * A_QKV + 2 * B_WIDTH + 2 * D_MODEL
X_HEADS = 4
X_HEAD_DIM = D_MODEL // X_HEADS
N_MEM = 256
N_EXPERT_GROUPS = 4
EXPERTS_PER_GROUP = 8
N_EXPERTS = N_EXPERT_GROUPS * EXPERTS_PER_GROUP
TOP_K = 2
D_EXPERT = 512
ROW_BLK = 256
ALPHA = (2 * DEPTH) ** 0.25
LN_EPS = 1e-5
NEG_INF = -1e30

LANES = 128
COL_TILE = 512
N_COL_TILES = C_IN // COL_TILE
T_GELU_END = 4
T_SIG_END = 8
T_Q0 = 8
T_K0 = 11
T_V0 = 14
VMEM_LIMIT = 56 * 1024 * 1024


def _params(sem, vmem=VMEM_LIMIT):
    return pltpu.CompilerParams(dimension_semantics=sem, vmem_limit_bytes=vmem)


def _layer_norm(y, g, b):
    mu = jnp.mean(y, axis=-1, keepdims=True)
    yc = y - mu
    var = jnp.mean(yc * yc, axis=-1, keepdims=True)
    return yc * lax.rsqrt(var + LN_EPS) * g + b


def _gelu_tanh(x):
    return 0.5 * x * (1.0 + jnp.tanh(0.7978845608028654 * (x + 0.044715 * (x * x * x))))


def _inproj_kernel(x_ref, w_ref, b_ref, cos_ref, sin_ref, o_ref, xb_ref):
    j = pl.program_id(1)

    @pl.when(j == 0)
    def _():
        xb_ref[...] = x_ref[...].astype(BF16)

    acc = jnp.dot(xb_ref[...], w_ref[...], preferred_element_type=F32) + b_ref[...]

    @pl.when(j < T_GELU_END)
    def _():
        o_ref[...] = _gelu_tanh(acc).astype(o_ref.dtype)

    @pl.when((j >= T_GELU_END) & (j < T_SIG_END))
    def _():
        o_ref[...] = jax.nn.sigmoid(acc).astype(o_ref.dtype)

    @pl.when((j >= T_Q0) & (j < T_V0))
    def _():
        reps = COL_TILE // LANES
        cos = jnp.concatenate([cos_ref[...]] * reps, axis=1)
        sin = jnp.concatenate([sin_ref[...]] * reps, axis=1)
        lane = lax.broadcasted_iota(jnp.int32, acc.shape, 1)
        first_half = (lane % A_HEAD_DIM) < (A_HEAD_DIM // 2)
        from_right = pltpu.roll(acc, COL_TILE - A_HEAD_DIM // 2, 1)
        from_left = pltpu.roll(acc, A_HEAD_DIM // 2, 1)
        swapped = jnp.where(first_half, from_right, from_left)
        rot = acc * cos + swapped * sin
        scale = jnp.where(j < T_K0, A_HEAD_DIM ** -0.5, 1.0).astype(F32)
        o_ref[...] = (rot * scale).astype(o_ref.dtype)

    @pl.when(j >= T_V0)
    def _():
        o_ref[...] = acc.astype(o_ref.dtype)


def _inproj(x, w, b, cos_t, sin_t, seq, tm):
    t = x.shape[0]
    assert t % tm == 0 and seq % tm == 0
    spt = seq // tm
    return pl.pallas_call(
        _inproj_kernel,
        grid=(t // tm, N_COL_TILES),
        in_specs=[
            pl.BlockSpec((tm, D_MODEL), lambda i, j: (i, 0)),
            pl.BlockSpec((D_MODEL, COL_TILE), lambda i, j: (0, j)),
            pl.BlockSpec((1, COL_TILE), lambda i, j: (0, j)),
            pl.BlockSpec((tm, LANES), lambda i, j: (i % spt, 0)),
            pl.BlockSpec((tm, LANES), lambda i, j: (i % spt, 0)),
        ],
        out_specs=pl.BlockSpec((tm, COL_TILE), lambda i, j: (i, j)),
        out_shape=jax.ShapeDtypeStruct((t, C_IN), BF16),
        scratch_shapes=[pltpu.VMEM((tm, D_MODEL), BF16)],
        compiler_params=_params(("parallel", "arbitrary")),
        name="inproj",
    )(x, w, b, cos_t, sin_t)


QB = 128
KW = QB + 2 * RADIUS


def _attn_kernel(q_ref, kc_ref, kp_ref, kn_ref, vc_ref, vp_ref, vn_ref, o_ref, lse_ref,
                 kbuf, vbuf, *, tq, length):
    i = pl.program_id(2)
    kbuf[0:RADIUS, :] = kp_ref[0]
    kbuf[RADIUS:RADIUS + tq, :] = kc_ref[0]
    kbuf[RADIUS + tq:, :] = kn_ref[0]
    vbuf[0:RADIUS, :] = vp_ref[0]
    vbuf[RADIUS:RADIUS + tq, :] = vc_ref[0]
    vbuf[RADIUS + tq:, :] = vn_ref[0]

    qi = lax.broadcasted_iota(jnp.int32, (QB, KW), 0)
    kj = lax.broadcasted_iota(jnp.int32, (QB, KW), 1)
    band = jnp.abs(kj - RADIUS - qi) <= RADIUS
    lane = lax.broadcasted_iota(jnp.int32, (QB, LANES), 1)
    low_half = lane < A_HEAD_DIM

    for jb in range(tq // QB):
        j0 = jb * QB
        kpos = i * tq + (j0 - RADIUS) + kj
        valid = band & (kpos >= 0) & (kpos < length)
        lse_tile = jnp.zeros((QB, LANES), F32)
        for p in range(A_HEADS // 2):
            cols = slice(p * LANES, (p + 1) * LANES)
            q2 = q_ref[0, j0:j0 + QB, cols]
            k2 = kbuf[j0:j0 + KW, cols]
            v2 = vbuf[j0:j0 + KW, cols]
            o_pair = jnp.zeros((QB, LANES), F32)
            for hh in range(2):
                hm = low_half if hh == 0 else jnp.logical_not(low_half)
                qm = jnp.where(hm, q2, jnp.zeros_like(q2))
                s = lax.dot_general(qm, k2, (((1,), (1,)), ((), ())), preferred_element_type=F32)
                s = jnp.where(valid, s, NEG_INF)
                m = jnp.max(s, axis=-1, keepdims=True)
                e = jnp.exp(s - m)
                den = jnp.sum(e, axis=-1, keepdims=True)
                pv = jnp.dot(e.astype(BF16), v2, preferred_element_type=F32)
                o_pair = jnp.where(hm, pv / den, o_pair)
                lse_tile = jnp.where(lane == (2 * p + hh), m + jnp.log(den), lse_tile)
            o_ref[0, j0:j0 + QB, cols] = o_pair.astype(o_ref.dtype)
        lse_ref[0, j0:j0 + QB, :] = lse_tile


def _dilated_attention(h, bsz, seq, g, dil):
    length = seq // dil
    tq = min(length, 512)
    assert length % tq == 0 and tq % QB == 0
    hb = RADIUS
    nhb = length // hb
    hv = h.reshape(bsz, length, dil * C_IN)
    qc, kc, vc = T_Q0 + g, T_K0 + g, T_V0 + g

    def cur(c):
        return pl.BlockSpec((1, tq, COL_TILE), lambda b, r, i: (b, i, r * N_COL_TILES + c))

    def prev(c):
        return pl.BlockSpec((1, hb, COL_TILE),
                            lambda b, r, i: (b, jnp.maximum(i * (tq // hb) - 1, 0), r * N_COL_TILES + c))

    def nxt(c):
        return pl.BlockSpec((1, hb, COL_TILE),
                            lambda b, r, i: (b, jnp.minimum((i + 1) * (tq // hb), nhb - 1), r * N_COL_TILES + c))

    o, lse = pl.pallas_call(
        functools.partial(_attn_kernel, tq=tq, length=length),
        grid=(bsz, dil, length // tq),
        in_specs=[cur(qc), cur(kc), prev(kc), nxt(kc), cur(vc), prev(vc), nxt(vc)],
        out_specs=[
            pl.BlockSpec((1, tq, A_OUT), lambda b, r, i: (b, i, r)),
            pl.BlockSpec((1, tq, LANES), lambda b, r, i: (b, i, r)),
        ],
        out_shape=[
            jax.ShapeDtypeStruct((bsz, length, dil * A_OUT), BF16),
            jax.ShapeDtypeStruct((bsz, length, dil * LANES), F32),
        ],
        scratch_shapes=[pltpu.VMEM((tq + 2 * RADIUS, COL_TILE), BF16),
                        pltpu.VMEM((tq + 2 * RADIUS, COL_TILE), BF16)],
        compiler_params=_params(("parallel", "parallel", "arbitrary")),
        name=f"dilated_attn_d{dil}",
    )(hv, hv, hv, hv, hv, hv, hv)
    return o.reshape(bsz * seq, A_OUT), lse.reshape(bsz * seq, LANES)


def _split_dot(w, e_ref):
    hi = w.astype(BF16)
    lo = (w - hi.astype(F32)).astype(BF16)
    e = e_ref[...]
    return (jnp.dot(hi, e, preferred_element_type=F32) + jnp.dot(lo, e, preferred_element_type=F32))


def _mix_kernel(o1_ref, o2_ref, o3_ref, l1_ref, l2_ref, l3_ref, zu_ref, zv_ref, g0_ref, g1_ref, x_ref,
                wsp_ref, bsp_ref, gmg_ref, gmb_ref, wa_ref, wb_ref, wo_ref, ebc_ref, lng_ref, lnb_ref,
                out_ref, vn_ref, bb_ref, *, ts):
    l1, l2, l3 = l1_ref[...], l2_ref[...], l3_ref[...]
    mx = jnp.maximum(jnp.maximum(l1, l2), l3)
    e1, e2, e3 = jnp.exp(l1 - mx), jnp.exp(l2 - mx), jnp.exp(l3 - mx)
    inv = 1.0 / (e1 + e2 + e3)
    a = (_split_dot(e1 * inv, ebc_ref) * o1_ref[...].astype(F32)
         + _split_dot(e2 * inv, ebc_ref) * o2_ref[...].astype(F32)
         + _split_dot(e3 * inv, ebc_ref) * o3_ref[...].astype(F32))
    ya = jnp.dot(a.astype(BF16), wa_ref[...], preferred_element_type=F32)

    vn_ref[...] = _layer_norm(zv_ref[...].astype(F32), gmg_ref[...], gmb_ref[...]).astype(BF16)
    gw = B_WIDTH // B_GROUPS
    for c in range(ts // CHUNK):
        rows = slice(c * CHUNK, (c + 1) * CHUNK)
        for g in range(B_GROUPS):
            cols = slice(g * gw, (g + 1) * gw)
            vm = jnp.dot(wsp_ref[g], vn_ref[rows, cols], preferred_element_type=F32) + bsp_ref[:, cols]
            bb_ref[rows, cols] = (zu_ref[rows, cols].astype(F32) * vm).astype(BF16)
    yb = jnp.dot(bb_ref[...], wb_ref[...], preferred_element_type=F32)

    merged = g0_ref[...].astype(F32) * ya + g1_ref[...].astype(F32) * yb
    z = jnp.dot(merged.astype(BF16), wo_ref[...], preferred_element_type=F32)
    out_ref[...] = _layer_norm(ALPHA * x_ref[...] + z, lng_ref[...], lnb_ref[...])


def _mix(h, outs, lses, x, lw, ts):
    t = x.shape[0]
    assert t % ts == 0 and ts % CHUNK == 0
    row = lambda w: pl.BlockSpec((ts, w), lambda i: (i, 0))
    hcol = lambda c: pl.BlockSpec((ts, D_MODEL), lambda i: (i, c))
    full = lambda a: pl.BlockSpec(a.shape, lambda i: (0,) * a.ndim)
    consts = [lw["w_sp"], lw["b_sp_full"], lw["gm_ln_g"], lw["gm_ln_b"], lw["w_a_out"], lw["w_b_out"],
              lw["w_o"], lw["head_bcast"], lw["ln1_g"], lw["ln1_b"]]
    return pl.pallas_call(
        functools.partial(_mix_kernel, ts=ts),
        grid=(t // ts,),
        in_specs=[row(A_OUT)] * 3 + [row(LANES)] * 3 + [hcol(0), hcol(1), hcol(2), hcol(3), row(D_MODEL)]
        + [full(a) for a in consts],
        out_specs=row(D_MODEL),
        out_shape=jax.ShapeDtypeStruct((t, D_MODEL), F32),
        scratch_shapes=[pltpu.VMEM((ts, B_WIDTH), BF16), pltpu.VMEM((ts, B_WIDTH), BF16)],
        compiler_params=_params(("parallel",)),
        name="branch_mix",
    )(*outs, *lses, h, h, h, h, x, *consts)


def _matmul_kernel(x_ref, w_ref, o_ref):
    o_ref[...] = jnp.dot(x_ref[...].astype(BF16), w_ref[...], preferred_element_type=F32).astype(o_ref.dtype)


def _matmul(x, w, tm, tn, out_dtype):
    m, k = x.shape
    n = w.shape[1]
    assert m % tm == 0 and n % tn == 0
    return pl.pallas_call(
        _matmul_kernel,
        grid=(m // tm, n // tn),
        in_specs=[pl.BlockSpec((tm, k), lambda i, j: (i, 0)), pl.BlockSpec((k, tn), lambda i, j: (0, j))],
        out_specs=pl.BlockSpec((tm, tn), lambda i, j: (i, j)),
        out_shape=jax.ShapeDtypeStruct((m, n), out_dtype),
        compiler_params=_params(("parallel", "parallel")),
        name="matmul",
    )(x, w)


def _xattn_kernel(x_ref, k_ref, v_ref, wq_ref, wo_ref, lng_ref, lnb_ref, wrh_ref, wrl_ref, br_ref,
                  x2_ref, ids_ref, wts_ref, o_scr):
    x = x_ref[...]
    q = jnp.dot(x.astype(BF16), wq_ref[...], preferred_element_type=F32)
    q = (q * (X_HEAD_DIM ** -0.5)).astype(BF16)
    for h in range(X_HEADS):
        cols = slice(h * X_HEAD_DIM, (h + 1) * X_HEAD_DIM)
        s = lax.dot_general(q[:, cols], k_ref[:, cols], (((1,), (1,)), ((), ())), preferred_element_type=F32)
        m = jnp.max(s, axis=-1, keepdims=True)
        e = jnp.exp(s - m)
        p = e / jnp.sum(e, axis=-1, keepdims=True)
        o_scr[:, cols] = jnp.dot(p.astype(BF16), v_ref[:, cols], preferred_element_type=F32).astype(BF16)
    z = jnp.dot(o_scr[...], wo_ref[...], preferred_element_type=F32)
    x2 = _layer_norm(ALPHA * x + z, lng_ref[...], lnb_ref[...])
    x2_ref[...] = x2

    xh = x2.astype(BF16)
    xl = (x2 - xh.astype(F32)).astype(BF16)
    logits = (jnp.dot(xh, wrh_ref[...], preferred_element_type=F32)
              + jnp.dot(xl, wrh_ref[...], preferred_element_type=F32)
              + jnp.dot(xh, wrl_ref[...], preferred_element_type=F32)) + br_ref[...]
    lane = lax.broadcasted_iota(jnp.int32, logits.shape, 1)
    big = jnp.int32(1 << 20)
    is_g = lane < N_EXPERT_GROUPS
    gl = jnp.where(is_g, logits, -jnp.inf)
    gmax = jnp.max(gl, axis=-1, keepdims=True)
    g_sel = jnp.min(jnp.where(gl == gmax, lane, big), axis=-1, keepdims=True)
    p_group = 1.0 / jnp.sum(jnp.where(is_g, jnp.exp(logits - gmax), 0.0), axis=-1, keepdims=True)
    lo = N_EXPERT_GROUPS + g_sel * EXPERTS_PER_GROUP
    el = jnp.where((lane >= lo) & (lane < lo + EXPERTS_PER_GROUP), logits, -jnp.inf)
    v1 = jnp.max(el, axis=-1, keepdims=True)
    i1 = jnp.min(jnp.where(el == v1, lane, big), axis=-1, keepdims=True)
    el2 = jnp.where(lane == i1, -jnp.inf, el)
    v2 = jnp.max(el2, axis=-1, keepdims=True)
    i2 = jnp.min(jnp.where(el2 == v2, lane, big), axis=-1, keepdims=True)
    t2 = jnp.exp(v2 - v1)
    w1 = p_group / (1.0 + t2)
    w2 = p_group * t2 / (1.0 + t2)
    ids_ref[...] = jnp.where(lane == 0, i1 - N_EXPERT_GROUPS, jnp.where(lane == 1, i2 - N_EXPERT_GROUPS, 0))
    wts_ref[...] = jnp.where(lane == 0, w1, jnp.where(lane == 1, w2, 0.0))


def _xattn(x1, kv, lw, bsz, seq, ts):
    t = x1.shape[0]
    assert seq % ts == 0
    spt = seq // ts
    row = lambda w: pl.BlockSpec((ts, w), lambda b, i: (b * spt + i, 0))
    full = lambda a: pl.BlockSpec(a.shape, lambda b, i: (0,) * a.ndim)
    consts = [lw["w_xq"], lw["w_xo"], lw["ln2_g"], lw["ln2_b"], lw["w_r_hi"], lw["w_r_lo"], lw["b_r"]]
    return pl.pallas_call(
        _xattn_kernel,
        grid=(bsz, spt),
        in_specs=[row(D_MODEL),
                  pl.BlockSpec((N_MEM, D_MODEL), lambda b, i: (b, 0)),
                  pl.BlockSpec((N_MEM, D_MODEL), lambda b, i: (b, 1))] + [full(a) for a in consts],
        out_specs=[row(D_MODEL), row(LANES), row(LANES)],
        out_shape=[jax.ShapeDtypeStruct((t, D_MODEL), F32),
                   jax.ShapeDtypeStruct((t, LANES), jnp.int32),
                   jax.ShapeDtypeStruct((t, LANES), F32)],
        scratch_shapes=[pltpu.VMEM((ts, D_MODEL), BF16)],
        compiler_params=_params(("parallel", "parallel")),
        name="cross_attn_router",
    )(x1, kv, kv, *consts)


def _row_copy(src, dst, sem):
    return pltpu.make_async_copy(src, dst, sem)


def _dispatch_kernel(pos_ref, x_hbm, xs_in_hbm, xs_hbm, sem, *, tt):
    del xs_in_hbm
    base = pl.program_id(0) * tt

    def start(t, c):
        src = x_hbm.at[pl.ds(base + t, 1)]
        for k in range(TOP_K):
            _row_copy(src, xs_hbm.at[pl.ds(pos_ref[0, 0, TOP_K * t + k], 1)], sem).start()
        return c

    lax.fori_loop(0, tt, start, 0)

    def wait(t, c):
        for k in range(TOP_K):
            _row_copy(x_hbm.at[pl.ds(0, 1)], xs_hbm.at[pl.ds(0, 1)], sem).wait()
        return c

    lax.fori_loop(0, tt, wait, 0)


def _dispatch(x2, pos, rows, tt):
    t = x2.shape[0]
    assert t % tt == 0
    xs0 = jnp.zeros((rows, D_MODEL), F32)
    return pl.pallas_call(
        functools.partial(_dispatch_kernel, tt=tt),
        grid=(t // tt,),
        in_specs=[pl.BlockSpec((1, 1, TOP_K * tt), lambda i: (i, 0, 0), memory_space=pltpu.SMEM),
                  pl.BlockSpec(memory_space=pl.ANY),
                  pl.BlockSpec(memory_space=pl.ANY)],
        out_specs=pl.BlockSpec(memory_space=pl.ANY),
        out_shape=jax.ShapeDtypeStruct((rows, D_MODEL), F32),
        scratch_shapes=[pltpu.SemaphoreType.DMA(())],
        input_output_aliases={2: 0},
        compiler_params=pltpu.CompilerParams(dimension_semantics=("arbitrary",), has_side_effects=True),
        name="moe_dispatch",
    )(pos.reshape(t // tt, 1, TOP_K * tt), x2, xs0)


def _expert_kernel(be_ref, nu_ref, xs_ref, wgu_ref, wd_ref, ys_ref):
    used = pl.program_id(0) < nu_ref[0]

    @pl.when(used)
    def _():
        xb = xs_ref[...].astype(BF16)
        gu = jnp.dot(xb, wgu_ref[0], preferred_element_type=F32)
        gate, up = gu[:, :D_EXPERT], gu[:, D_EXPERT:]
        hmid = (gate * jax.nn.sigmoid(gate) * up).astype(BF16)
        ys_ref[...] = jnp.dot(hmid, wd_ref[0], preferred_element_type=F32)

    @pl.when(jnp.logical_not(used))
    def _():
        ys_ref[...] = jnp.zeros_like(ys_ref)


def _experts(xs, block_e, n_used, w_gu, w_down):
    rows = xs.shape[0]
    n_blocks = rows // ROW_BLK
    blk = lambda b, be, nu: (jnp.minimum(b, nu[0] - 1), 0)
    out_blk = lambda b, be, nu: (b, 0)
    return pl.pallas_call(
        _expert_kernel,
        grid_spec=pltpu.PrefetchScalarGridSpec(
            num_scalar_prefetch=2,
            grid=(n_blocks,),
            in_specs=[
                pl.BlockSpec((ROW_BLK, D_MODEL), blk),
                pl.BlockSpec((1, D_MODEL, 2 * D_EXPERT), lambda b, be, nu: (be[jnp.minimum(b, nu[0] - 1)], 0, 0)),
                pl.BlockSpec((1, D_EXPERT, D_MODEL), lambda b, be, nu: (be[jnp.minimum(b, nu[0] - 1)], 0, 0)),
            ],
            out_specs=pl.BlockSpec((ROW_BLK, D_MODEL), out_blk),
        ),
        out_shape=jax.ShapeDtypeStruct((rows, D_MODEL), F32),
        compiler_params=_params(("arbitrary",)),
        name="moe_experts",
    )(block_e, n_used, xs, w_gu, w_down)


def _combine_kernel(pos_ref, wts_ref, x_ref, ys_hbm, lng_ref, lnb_ref, out_ref, ybuf, sem, *, tc):
    def start(t, c):
        for k in range(TOP_K):
            _row_copy(ys_hbm.at[pl.ds(pos_ref[0, 0, TOP_K * t + k], 1)], ybuf.at[k, pl.ds(t, 1)], sem).start()
        return c

    lax.fori_loop(0, tc, start, 0)

    def wait(t, c):
        for k in range(TOP_K):
            _row_copy(ys_hbm.at[pl.ds(0, 1)], ybuf.at[k, pl.ds(0, 1)], sem).wait()
        return c

    lax.fori_loop(0, tc, wait, 0)
    w = wts_ref[...]
    moe = w[:, 0:1] * ybuf[0] + w[:, 1:2] * ybuf[1]
    out_ref[...] = _layer_norm(ALPHA * x_ref[...] + moe, lng_ref[...], lnb_ref[...])


def _combine(x2, wts, ys, pos, ln_g, ln_b, tc):
    t = x2.shape[0]
    assert t % tc == 0
    row = lambda w: pl.BlockSpec((tc, w), lambda i: (i, 0))
    return pl.pallas_call(
        functools.partial(_combine_kernel, tc=tc),
        grid=(t // tc,),
        in_specs=[pl.BlockSpec((1, 1, TOP_K * tc), lambda i: (i, 0, 0), memory_space=pltpu.SMEM),
                  row(LANES), row(D_MODEL),
                  pl.BlockSpec(memory_space=pl.ANY),
                  pl.BlockSpec((1, D_MODEL), lambda i: (0, 0)),
                  pl.BlockSpec((1, D_MODEL), lambda i: (0, 0))],
        out_specs=row(D_MODEL),
        out_shape=jax.ShapeDtypeStruct((t, D_MODEL), F32),
        scratch_shapes=[pltpu.VMEM((TOP_K, tc, D_MODEL), F32), pltpu.SemaphoreType.DMA(())],
        compiler_params=_params(("arbitrary",)),
        name="moe_combine",
    )(pos.reshape(t // tc, 1, TOP_K * tc), wts, x2, ys, ln_g, ln_b)


def _moe_plan(ids, t):
    n_assign = t * TOP_K
    n_blocks = -(-n_assign // ROW_BLK) + N_EXPERTS
    e = ids[:, :TOP_K].reshape(-1)
    onehot = (e[:, None] == jnp.arange(N_EXPERTS, dtype=jnp.int32)[None, :]).astype(jnp.int32)
    csum = jnp.cumsum(onehot, axis=0)
    rank = jnp.take_along_axis(csum, e[:, None], axis=1)[:, 0] - 1
    counts = csum[-1]
    padded = ((counts + ROW_BLK - 1) // ROW_BLK) * ROW_BLK
    pad_end = jnp.cumsum(padded)
    pad_start = pad_end - padded
    pos = (pad_start[e] + rank).astype(jnp.int32)
    block_e = jnp.minimum(jnp.searchsorted(pad_end, jnp.arange(n_blocks, dtype=jnp.int32) * ROW_BLK, side="right"),
                          N_EXPERTS - 1).astype(jnp.int32)
    n_used = (pad_end[-1:] // ROW_BLK).astype(jnp.int32)
    return pos, block_e, n_used, n_blocks * ROW_BLK


def _rope_tables(seq):
    half = A_HEAD_DIM // 2
    inv = ROPE_THETA ** (-jnp.arange(half, dtype=F32) / half)
    ang = jnp.arange(seq).astype(F32)[:, None] * inv[None, :]
    cos, sin = jnp.cos(ang), jnp.sin(ang)
    reps = LANES // A_HEAD_DIM
    cos_t = jnp.tile(cos, (1, 2 * reps))
    sin_t = jnp.tile(jnp.concatenate([-sin, sin], axis=1), (1, reps))
    return cos_t, sin_t


def _prep_layer(l, w_in, b_in, gm_ln_g, gm_ln_b, w_sp, b_sp, w_a_out, w_b_out, w_o, ln1_g, ln1_b,
                w_xq, w_xkv, w_xo, ln2_g, ln2_b, w_rg, b_rg, w_re, b_re, w_gu, w_down, ln3_g, ln3_b):
    q0, k0, v0, u0, z0, g0 = 0, A_QKV, 2 * A_QKV, 3 * A_QKV, 3 * A_QKV + B_WIDTH, 3 * A_QKV + 2 * B_WIDTH
    order = [(u0, z0), (z0, g0), (g0, C_IN), (q0, k0), (k0, v0), (v0, u0)]
    w_in_p = jnp.concatenate([w_in[l][:, a:b] for a, b in order], axis=1).astype(BF16)
    b_in_p = jnp.concatenate([b_in[l][a:b] for a, b in order])[None, :]
    w_r = jnp.zeros((D_MODEL, LANES), F32)
    w_r = w_r.at[:, :N_EXPERT_GROUPS].set(w_rg[l]).at[:, N_EXPERT_GROUPS:N_EXPERT_GROUPS + N_EXPERTS].set(w_re[l])
    b_r = jnp.zeros((1, LANES), F32)
    b_r = b_r.at[0, :N_EXPERT_GROUPS].set(b_rg[l]).at[0, N_EXPERT_GROUPS:N_EXPERT_GROUPS + N_EXPERTS].set(b_re[l])
    w_r_hi = w_r.astype(BF16)
    head = jnp.arange(A_OUT, dtype=jnp.int32)[None, :] // A_HEAD_DIM
    head_bcast = (jnp.arange(LANES, dtype=jnp.int32)[:, None] == head).astype(BF16)
    row = lambda a: a[l][None, :]
    return dict(
        w_in=w_in_p, b_in=b_in_p,
        gm_ln_g=row(gm_ln_g), gm_ln_b=row(gm_ln_b),
        w_sp=w_sp[l].astype(BF16),
        b_sp_full=jnp.repeat(b_sp[l].T, B_WIDTH // B_GROUPS, axis=1),
        w_a_out=w_a_out[l].astype(BF16), w_b_out=w_b_out[l].astype(BF16), w_o=w_o[l].astype(BF16),
        head_bcast=head_bcast, ln1_g=row(ln1_g), ln1_b=row(ln1_b),
        w_xq=w_xq[l].astype(BF16), w_xkv=w_xkv[l].astype(BF16), w_xo=w_xo[l].astype(BF16),
        ln2_g=row(ln2_g), ln2_b=row(ln2_b),
        w_r_hi=w_r_hi, w_r_lo=(w_r - w_r_hi.astype(F32)).astype(BF16), b_r=b_r,
        w_gu=w_gu[l].astype(BF16), w_down=w_down[l].astype(BF16),
        ln3_g=row(ln3_g), ln3_b=row(ln3_b),
    )


def _trunk(x, mem, layers, tiles):
    bsz, seq, _ = x.shape
    t = bsz * seq
    x = x.reshape(t, D_MODEL)
    memf = mem.reshape(bsz * N_MEM, D_MODEL)
    cos_t, sin_t = _rope_tables(seq)
    for lw in layers:
        h = _inproj(x, lw["w_in"], lw["b_in"], cos_t, sin_t, seq, tiles["inproj"])
        outs, lses = [], []
        for g, (_, dil) in enumerate(A_PATTERNS):
            o, lse = _dilated_attention(h, bsz, seq, g, dil)
            outs.append(o)
            lses.append(lse)
        x1 = _mix(h, outs, lses, x, lw, tiles["mix"])
        kv = _matmul(memf, lw["w_xkv"], N_MEM, D_MODEL, BF16)
        x2, ids, wts = _xattn(x1, kv, lw, bsz, seq, tiles["xattn"])
        pos, block_e, n_used, rows = _moe_plan(ids, t)
        xs = _dispatch(x2, pos, rows, tiles["dispatch"])
        ys = _experts(xs, block_e, n_used, lw["w_gu"], lw["w_down"])
        x = _combine(x2, wts, ys, pos, lw["ln3_g"], lw["ln3_b"], tiles["combine"])
    return x.reshape(bsz, seq, D_MODEL)


TILES = dict(inproj=1024, mix=256, xattn=512, dispatch=1024, combine=256)


def kernel(x_prompt, x_sample, mem_prompt, mem_sample, w_in, b_in, gm_ln_g, gm_ln_b, w_sp, b_sp, w_a_out, w_b_out, w_o, ln1_g, ln1_b, w_xq, w_xkv, w_xo, ln2_g, ln2_b, w_rg, b_rg, w_re, b_re, w_gu, w_down, ln3_g, ln3_b):
    params = (w_in, b_in, gm_ln_g, gm_ln_b, w_sp, b_sp, w_a_out, w_b_out, w_o, ln1_g, ln1_b,
              w_xq, w_xkv, w_xo, ln2_g, ln2_b, w_rg, b_rg, w_re, b_re, w_gu, w_down, ln3_g, ln3_b)
    layers = [_prep_layer(l, *params) for l in range(DEPTH)]
    y_prompt = _trunk(x_prompt, mem_prompt, layers, TILES)
    y_sample = _trunk(x_sample, mem_sample, layers, TILES)
    return (y_prompt, y_sample)
```

```python
import functools

import jax
import jax.numpy as jnp
from jax import lax
from jax.experimental import pallas as pl
from jax.experimental.pallas import tpu as pltpu

F32 = jnp.float32
BF16 = jnp.bfloat16

D_MODEL = 1024
DEPTH = 2
A_PATTERNS = ((128, 1), (512, 4), (2048, 16))
A_GROUPS = 3
A_HEADS = 8
A_HEAD_DIM = 64
A_QKV = A_GROUPS * A_HEADS * A_HEAD_DIM
A_OUT = A_HEADS * A_HEAD_DIM
RADIUS = 64
ROPE_THETA = 10000.0
B_WIDTH = 1024
B_GROUPS = 8
CHUNK = 128
C_IN = 3 * A_QKV + 2 * B_WIDTH + 2 * D_MODEL
X_HEADS = 4
X_HEAD_DIM = D_MODEL // X_HEADS
N_MEM = 256
N_EXPERT_GROUPS = 4
EXPERTS_PER_GROUP = 8
N_EXPERTS = N_EXPERT_GROUPS * EXPERTS_PER_GROUP
TOP_K = 2
D_EXPERT = 512
ROW_BLK = 256
ALPHA = (2 * DEPTH) ** 0.25
LN_EPS = 1e-5
NEG_INF = -1e30

LANES = 128
COL_TILE = 512
TOK_COLS = 2 * B_WIDTH + 2 * D_MODEL
TOK_TILES = TOK_COLS // COL_TILE
T_GELU_END = 4
ROW_TILE = 2048
VMEM_LIMIT = 56 * 1024 * 1024


def _params(sem, vmem=VMEM_LIMIT):
    return pltpu.CompilerParams(dimension_semantics=sem, vmem_limit_bytes=vmem)


def _layer_norm(y, g, b):
    mu = jnp.mean(y, axis=-1, keepdims=True)
    yc = y - mu
    var = jnp.mean(yc * yc, axis=-1, keepdims=True)
    return yc * lax.rsqrt(var + LN_EPS) * g + b


def _gelu_tanh(x):
    return 0.5 * x * (1.0 + jnp.tanh(0.7978845608028654 * (x + 0.044715 * (x * x * x))))


def _inproj_tok_kernel(x_ref, w_ref, b_ref, o_ref):
    j = pl.program_id(1)
    acc = jnp.dot(x_ref[...], w_ref[...], preferred_element_type=F32) + b_ref[...]

    @pl.when(j < T_GELU_END)
    def _():
        o_ref[...] = _gelu_tanh(acc).astype(o_ref.dtype)

    @pl.when(j >= T_GELU_END)
    def _():
        o_ref[...] = jax.nn.sigmoid(acc).astype(o_ref.dtype)


def _inproj_tok(xb, w, b, tm):
    t = xb.shape[0]
    assert t % tm == 0
    return pl.pallas_call(
        _inproj_tok_kernel,
        grid=(t // tm, TOK_TILES),
        in_specs=[
            pl.BlockSpec((tm, D_MODEL), lambda i, j: (i, 0)),
            pl.BlockSpec((D_MODEL, COL_TILE), lambda i, j: (0, j)),
            pl.BlockSpec((1, COL_TILE), lambda i, j: (0, j)),
        ],
        out_specs=pl.BlockSpec((tm, COL_TILE), lambda i, j: (i, j)),
        out_shape=jax.ShapeDtypeStruct((t, TOK_COLS), BF16),
        compiler_params=_params(("parallel", "arbitrary")),
        name="inproj_tok",
    )(xb, w, b)


EPI_ROWS = 256


def _inproj_qkv_kernel(x_ref, w_ref, b_ref, cos_ref, sin_ref, o_ref, acc_ref, perm_ref, *, tm, dil):
    j = pl.program_id(1)
    acc_ref[...] = jnp.dot(x_ref[...], w_ref[...], preferred_element_type=F32)
    reps = COL_TILE // LANES
    lane = lax.broadcasted_iota(jnp.int32, (EPI_ROWS, COL_TILE), 1)
    first_half = (lane % A_HEAD_DIM) < (A_HEAD_DIM // 2)
    scale = jnp.where(j == 0, A_HEAD_DIM ** -0.5, 1.0).astype(F32)

    def emit(rows, val):
        if dil == 1:
            o_ref[0, rows, :] = val.astype(o_ref.dtype)
        else:
            for s in range(reps):
                perm_ref[s, rows, :] = val[:, s * LANES:(s + 1) * LANES]

    @pl.when(j < 2)
    def _():
        for c in range(tm // EPI_ROWS):
            rows = slice(c * EPI_ROWS, (c + 1) * EPI_ROWS)
            t = acc_ref[rows, :] + b_ref[...]
            cos = jnp.concatenate([cos_ref[rows, :]] * reps, axis=1)
            sin = jnp.concatenate([sin_ref[rows, :]] * reps, axis=1)
            from_right = pltpu.roll(t, COL_TILE - A_HEAD_DIM // 2, 1)
            from_left = pltpu.roll(t, A_HEAD_DIM // 2, 1)
            swapped = jnp.where(first_half, from_right, from_left)
            emit(rows, (t * cos + swapped * sin) * scale)

    @pl.when(j == 2)
    def _():
        for c in range(tm // EPI_ROWS):
            rows = slice(c * EPI_ROWS, (c + 1) * EPI_ROWS)
            emit(rows, acc_ref[rows, :] + b_ref[...])

    if dil > 1:
        rows_per = tm // dil
        for r in range(dil):
            for s in range(reps):
                o_ref[r, :, s * LANES:(s + 1) * LANES] = (
                    perm_ref[s, pl.ds(r, rows_per, stride=dil), :].astype(o_ref.dtype))


def _inproj_qkv(xb, w, b, cos_t, sin_t, seq, g, dil, tm):
    t = xb.shape[0]
    assert t % tm == 0 and seq % tm == 0 and tm % EPI_ROWS == 0 and tm % (dil * QB) == 0
    spt = seq // tm
    col0 = TOK_TILES + g
    return pl.pallas_call(
        functools.partial(_inproj_qkv_kernel, tm=tm, dil=dil),
        grid=(t // tm, 3),
        in_specs=[
            pl.BlockSpec((tm, D_MODEL), lambda i, j: (i, 0)),
            pl.BlockSpec((D_MODEL, COL_TILE), lambda i, j: (0, col0 + A_GROUPS * j)),
            pl.BlockSpec((1, COL_TILE), lambda i, j: (0, col0 + A_GROUPS * j)),
            pl.BlockSpec((tm, LANES), lambda i, j: (i % spt, 0)),
            pl.BlockSpec((tm, LANES), lambda i, j: (i % spt, 0)),
        ],
        out_specs=pl.BlockSpec((None, None, dil, tm // dil, COL_TILE), lambda i, j: (i, j, 0, 0, 0)),
        out_shape=jax.ShapeDtypeStruct((t // tm, 3, dil, tm // dil, COL_TILE), BF16),
        scratch_shapes=[pltpu.VMEM((tm, COL_TILE), F32),
                        pltpu.VMEM((COL_TILE // LANES, tm if dil > 1 else 8, LANES), F32)],
        compiler_params=_params(("parallel", "arbitrary")),
        name=f"inproj_qkv_d{dil}",
    )(xb, w, b, cos_t, sin_t)


QB = 128
KW = QB + 2 * RADIUS


def _attn_kernel(q_ref, kc_ref, kp_ref, kn_ref, vc_ref, vp_ref, vn_ref, o_ref, lse_ref,
                 kbuf, vbuf, *, tq, length):
    i = pl.program_id(2)
    kbuf[0:RADIUS, :] = kp_ref[...]
    kbuf[RADIUS:RADIUS + tq, :] = kc_ref[...]
    kbuf[RADIUS + tq:, :] = kn_ref[...]
    vbuf[0:RADIUS, :] = vp_ref[...]
    vbuf[RADIUS:RADIUS + tq, :] = vc_ref[...]
    vbuf[RADIUS + tq:, :] = vn_ref[...]

    qi = lax.broadcasted_iota(jnp.int32, (QB, KW), 0)
    kj = lax.broadcasted_iota(jnp.int32, (QB, KW), 1)
    band = jnp.abs(kj - RADIUS - qi) <= RADIUS
    lane = lax.broadcasted_iota(jnp.int32, (QB, LANES), 1)
    low_half = lane < A_HEAD_DIM

    def block(jb, carry):
        j0 = pl.multiple_of(jb * QB, QB)
        kpos = i * tq + (j0 - RADIUS) + kj
        valid = band & (kpos >= 0) & (kpos < length)
        lse_tile = jnp.zeros((QB, LANES), F32)
        for p in range(A_HEADS // 2):
            cols = slice(p * LANES, (p + 1) * LANES)
            q2 = q_ref[pl.ds(j0, QB), cols]
            k2 = kbuf[pl.ds(j0, KW), cols]
            v2 = vbuf[pl.ds(j0, KW), cols]
            o_pair = jnp.zeros((QB, LANES), F32)
            for hh in range(2):
                hm = low_half if hh == 0 else jnp.logical_not(low_half)
                qm = jnp.where(hm, q2, jnp.zeros_like(q2))
                s = lax.dot_general(qm, k2, (((1,), (1,)), ((), ())), preferred_element_type=F32)
                s = jnp.where(valid, s, NEG_INF)
                m = jnp.max(s, axis=-1, keepdims=True)
                e = jnp.exp(s - m)
                den = jnp.sum(e, axis=-1, keepdims=True)
                pv = jnp.dot(e.astype(BF16), v2, preferred_element_type=F32)
                o_pair = jnp.where(hm, pv / den, o_pair)
                lse_tile = jnp.where(lane == (2 * p + hh), m + jnp.log(den), lse_tile)
            o_ref[pl.ds(j0, QB), cols] = o_pair.astype(o_ref.dtype)
        lse_ref[pl.ds(j0, QB), :] = lse_tile
        return carry

    lax.fori_loop(0, tq // QB, block, 0)


def _dilated_attention(qkv, bsz, seq, dil, tm):
    nt = qkv.shape[0]
    spt = seq // tm
    tq = tm // dil
    length = seq // dil
    assert tq % QB == 0
    hb = RADIUS
    last_hb = tq // hb - 1

    def cur(c):
        return pl.BlockSpec((None, None, None, tq, COL_TILE), lambda b, r, i: (b * spt + i, c, r, 0, 0))

    def prev(c):
        return pl.BlockSpec((None, None, None, hb, COL_TILE),
                            lambda b, r, i: (jnp.maximum(b * spt + i - 1, 0), c, r, last_hb, 0))

    def nxt(c):
        return pl.BlockSpec((None, None, None, hb, COL_TILE),
                            lambda b, r, i: (jnp.minimum(b * spt + i + 1, nt - 1), c, r, 0, 0))

    return pl.pallas_call(
        functools.partial(_attn_kernel, tq=tq, length=length),
        grid=(bsz, dil, spt),
        in_specs=[cur(0), cur(1), prev(1), nxt(1), cur(2), prev(2), nxt(2)],
        out_specs=[
            pl.BlockSpec((None, None, tq, A_OUT), lambda b, r, i: (b * spt + i, r, 0, 0)),
            pl.BlockSpec((None, None, tq, LANES), lambda b, r, i: (b * spt + i, r, 0, 0)),
        ],
        out_shape=[
            jax.ShapeDtypeStruct((nt, dil, tq, A_OUT), BF16),
            jax.ShapeDtypeStruct((nt, dil, tq, LANES), F32),
        ],
        scratch_shapes=[pltpu.VMEM((tq + 2 * RADIUS, COL_TILE), BF16),
                        pltpu.VMEM((tq + 2 * RADIUS, COL_TILE), BF16)],
        compiler_params=_params(("parallel", "parallel", "arbitrary")),
        name=f"dilated_attn_d{dil}",
    )(qkv, qkv, qkv, qkv, qkv, qkv, qkv)


def _split_dot(w, e_ref):
    hi = w.astype(BF16)
    lo = (w - hi.astype(F32)).astype(BF16)
    e = e_ref[...]
    return (jnp.dot(hi, e, preferred_element_type=F32) + jnp.dot(lo, e, preferred_element_type=F32))


def _interleave(src_ref, scr_ref, dil):
    if dil == 1:
        return src_ref[0].astype(F32)
    rows_per, width = src_ref.shape[1], src_ref.shape[2]
    slabs = width // LANES
    for r in range(dil):
        v = src_ref[r].astype(F32)
        for s in range(slabs):
            scr_ref[s, pl.ds(r, rows_per, stride=dil), :] = v[:, s * LANES:(s + 1) * LANES]
    return jnp.concatenate([scr_ref[s] for s in range(slabs)], axis=1)


def _mix_kernel(o1_ref, o2_ref, o3_ref, l1_ref, l2_ref, l3_ref, zu_ref, zv_ref, g0_ref, g1_ref, x_ref,
                wsp_ref, bsp_ref, gmg_ref, gmb_ref, wa_ref, wb_ref, wo_ref, ebc_ref, lng_ref, lnb_ref,
                out_ref, vn_ref, bb_ref, o_scr, l_scr, *, ts):
    dils = [d for _, d in A_PATTERNS]
    l1 = _interleave(l1_ref, l_scr.at[0], dils[0])
    l2 = _interleave(l2_ref, l_scr.at[1], dils[1])
    l3 = _interleave(l3_ref, l_scr.at[2], dils[2])
    mx = jnp.maximum(jnp.maximum(l1, l2), l3)
    e1, e2, e3 = jnp.exp(l1 - mx), jnp.exp(l2 - mx), jnp.exp(l3 - mx)
    inv = 1.0 / (e1 + e2 + e3)
    a = (_split_dot(e1 * inv, ebc_ref) * _interleave(o1_ref, o_scr.at[0], dils[0])
         + _split_dot(e2 * inv, ebc_ref) * _interleave(o2_ref, o_scr.at[1], dils[1])
         + _split_dot(e3 * inv, ebc_ref) * _interleave(o3_ref, o_scr.at[2], dils[2]))
    ya = jnp.dot(a.astype(BF16), wa_ref[...], preferred_element_type=F32)

    vn_ref[...] = _layer_norm(zv_ref[...].astype(F32), gmg_ref[...], gmb_ref[...]).astype(BF16)
    gw = B_WIDTH // B_GROUPS
    for c in range(ts // CHUNK):
        rows = slice(c * CHUNK, (c + 1) * CHUNK)
        for g in range(B_GROUPS):
            cols = slice(g * gw, (g + 1) * gw)
            vm = jnp.dot(wsp_ref[g], vn_ref[rows, cols], preferred_element_type=F32) + bsp_ref[:, cols]
            bb_ref[rows, cols] = (zu_ref[rows, cols].astype(F32) * vm).astype(BF16)
    yb = jnp.dot(bb_ref[...], wb_ref[...], preferred_element_type=F32)

    merged = g0_ref[...].astype(F32) * ya + g1_ref[...].astype(F32) * yb
    z = jnp.dot(merged.astype(BF16), wo_ref[...], preferred_element_type=F32)
    out_ref[...] = _layer_norm(ALPHA * x_ref[...] + z, lng_ref[...], lnb_ref[...])


def _mix(htok, outs, lses, x, lw, ts, tm):
    t = x.shape[0]
    assert t % ts == 0 and ts % CHUNK == 0 and tm % ts == 0
    sub = tm // ts
    row = lambda w: pl.BlockSpec((ts, w), lambda i: (i, 0))
    hcol = lambda c: pl.BlockSpec((ts, D_MODEL), lambda i: (i, c))
    full = lambda a: pl.BlockSpec(a.shape, lambda i: (0,) * a.ndim)

    def perm(w, dil):
        assert ts % (dil * 16) == 0
        return pl.BlockSpec((None, dil, ts // dil, w), lambda i: (i // sub, 0, i % sub, 0))

    dils = [d for _, d in A_PATTERNS]
    consts = [lw["w_sp"], lw["b_sp_full"], lw["gm_ln_g"], lw["gm_ln_b"], lw["w_a_out"], lw["w_b_out"],
              lw["w_o"], lw["head_bcast"], lw["ln1_g"], lw["ln1_b"]]
    return pl.pallas_call(
        functools.partial(_mix_kernel, ts=ts),
        grid=(t // ts,),
        in_specs=[perm(A_OUT, d) for d in dils] + [perm(LANES, d) for d in dils]
        + [hcol(0), hcol(1), hcol(2), hcol(3), row(D_MODEL)] + [full(a) for a in consts],
        out_specs=row(D_MODEL),
        out_shape=jax.ShapeDtypeStruct((t, D_MODEL), F32),
        scratch_shapes=[pltpu.VMEM((ts, B_WIDTH), BF16), pltpu.VMEM((ts, B_WIDTH), BF16),
                        pltpu.VMEM((A_GROUPS, A_OUT // LANES, ts, LANES), F32),
                        pltpu.VMEM((A_GROUPS, 1, ts, LANES), F32)],
        compiler_params=_params(("parallel",)),
        name="branch_mix",
    )(*outs, *lses, htok, htok, htok, htok, x, *consts)


def _matmul_kernel(x_ref, w_ref, o_ref):
    o_ref[...] = jnp.dot(x_ref[...].astype(BF16), w_ref[...], preferred_element_type=F32).astype(o_ref.dtype)


def _matmul(x, w, tm, tn, out_dtype):
    m, k = x.shape
    n = w.shape[1]
    assert m % tm == 0 and n % tn == 0
    return pl.pallas_call(
        _matmul_kernel,
        grid=(m // tm, n // tn),
        in_specs=[pl.BlockSpec((tm, k), lambda i, j: (i, 0)), pl.BlockSpec((k, tn), lambda i, j: (0, j))],
        out_specs=pl.BlockSpec((tm, tn), lambda i, j: (i, j)),
        out_shape=jax.ShapeDtypeStruct((m, n), out_dtype),
        compiler_params=_params(("parallel", "parallel")),
        name="matmul",
    )(x, w)


def _xattn_kernel(x_ref, k_ref, v_ref, wq_ref, wo_ref, lng_ref, lnb_ref, wrh_ref, wrl_ref, br_ref,
                  x2_ref, ids_ref, wts_ref, o_scr):
    x = x_ref[...]
    q = jnp.dot(x.astype(BF16), wq_ref[...], preferred_element_type=F32)
    q = (q * (X_HEAD_DIM ** -0.5)).astype(BF16)
    for h in range(X_HEADS):
        cols = slice(h * X_HEAD_DIM, (h + 1) * X_HEAD_DIM)
        s = lax.dot_general(q[:, cols], k_ref[:, cols], (((1,), (1,)), ((), ())), preferred_element_type=F32)
        m = jnp.max(s, axis=-1, keepdims=True)
        e = jnp.exp(s - m)
        p = e / jnp.sum(e, axis=-1, keepdims=True)
        o_scr[:, cols] = jnp.dot(p.astype(BF16), v_ref[:, cols], preferred_element_type=F32).astype(BF16)
    z = jnp.dot(o_scr[...], wo_ref[...], preferred_element_type=F32)
    x2 = _layer_norm(ALPHA * x + z, lng_ref[...], lnb_ref[...])
    x2_ref[...] = x2

    xh = x2.astype(BF16)
    xl = (x2 - xh.astype(F32)).astype(BF16)
    logits = (jnp.dot(xh, wrh_ref[...], preferred_element_type=F32)
              + jnp.dot(xl, wrh_ref[...], preferred_element_type=F32)
              + jnp.dot(xh, wrl_ref[...], preferred_element_type=F32)) + br_ref[...]
    lane = lax.broadcasted_iota(jnp.int32, logits.shape, 1)
    big = jnp.int32(1 << 20)
    is_g = lane < N_EXPERT_GROUPS
    gl = jnp.where(is_g, logits, -jnp.inf)
    gmax = jnp.max(gl, axis=-1, keepdims=True)
    g_sel = jnp.min(jnp.where(gl == gmax, lane, big), axis=-1, keepdims=True)
    p_group = 1.0 / jnp.sum(jnp.where(is_g, jnp.exp(logits - gmax), 0.0), axis=-1, keepdims=True)
    lo = N_EXPERT_GROUPS + g_sel * EXPERTS_PER_GROUP
    el = jnp.where((lane >= lo) & (lane < lo + EXPERTS_PER_GROUP), logits, -jnp.inf)
    v1 = jnp.max(el, axis=-1, keepdims=True)
    i1 = jnp.min(jnp.where(el == v1, lane, big), axis=-1, keepdims=True)
    el2 = jnp.where(lane == i1, -jnp.inf, el)
    v2 = jnp.max(el2, axis=-1, keepdims=True)
    i2 = jnp.min(jnp.where(el2 == v2, lane, big), axis=-1, keepdims=True)
    t2 = jnp.exp(v2 - v1)
    w1 = p_group / (1.0 + t2)
    w2 = p_group * t2 / (1.0 + t2)
    ids_ref[...] = jnp.where(lane == 0, i1 - N_EXPERT_GROUPS, jnp.where(lane == 1, i2 - N_EXPERT_GROUPS, 0))
    wts_ref[...] = jnp.where(lane == 0, w1, jnp.where(lane == 1, w2, 0.0))


def _xattn(x1, kv, lw, bsz, seq, ts):
    t = x1.shape[0]
    assert seq % ts == 0
    spt = seq // ts
    row = lambda w: pl.BlockSpec((ts, w), lambda b, i: (b * spt + i, 0))
    full = lambda a: pl.BlockSpec(a.shape, lambda b, i: (0,) * a.ndim)
    consts = [lw["w_xq"], lw["w_xo"], lw["ln2_g"], lw["ln2_b"], lw["w_r_hi"], lw["w_r_lo"], lw["b_r"]]
    return pl.pallas_call(
        _xattn_kernel,
        grid=(bsz, spt),
        in_specs=[row(D_MODEL),
                  pl.BlockSpec((N_MEM, D_MODEL), lambda b, i: (b, 0)),
                  pl.BlockSpec((N_MEM, D_MODEL), lambda b, i: (b, 1))] + [full(a) for a in consts],
        out_specs=[row(D_MODEL), row(LANES), row(LANES)],
        out_shape=[jax.ShapeDtypeStruct((t, D_MODEL), F32),
                   jax.ShapeDtypeStruct((t, LANES), jnp.int32),
                   jax.ShapeDtypeStruct((t, LANES), F32)],
        scratch_shapes=[pltpu.VMEM((ts, D_MODEL), BF16)],
        compiler_params=_params(("parallel", "parallel")),
        name="cross_attn_router",
    )(x1, kv, kv, *consts)


def _row_copy(src, dst, sem):
    return pltpu.make_async_copy(src, dst, sem)


def _dispatch_kernel(pos_ref, x_ref, xs_in_hbm, xs_hbm, sem, *, tt):
    del xs_in_hbm

    def start(t, c):
        src = x_ref.at[pl.ds(t, 1)]
        for k in range(TOP_K):
            _row_copy(src, xs_hbm.at[pl.ds(pos_ref[0, 0, TOP_K * t + k], 1)], sem).start()
        return c

    lax.fori_loop(0, tt, start, 0)

    def wait(t, c):
        for k in range(TOP_K):
            _row_copy(x_ref.at[pl.ds(0, 1)], xs_hbm.at[pl.ds(0, 1)], sem).wait()
        return c

    lax.fori_loop(0, tt, wait, 0)


def _dispatch(x2, pos, rows, tt):
    t = x2.shape[0]
    assert t % tt == 0
    xs0 = jnp.zeros((rows, D_MODEL), F32)
    return pl.pallas_call(
        functools.partial(_dispatch_kernel, tt=tt),
        grid=(t // tt,),
        in_specs=[pl.BlockSpec((1, 1, TOP_K * tt), lambda i: (i, 0, 0), memory_space=pltpu.SMEM),
                  pl.BlockSpec((tt, D_MODEL), lambda i: (i, 0)),
                  pl.BlockSpec(memory_space=pl.ANY)],
        out_specs=pl.BlockSpec(memory_space=pl.ANY),
        out_shape=jax.ShapeDtypeStruct((rows, D_MODEL), F32),
        scratch_shapes=[pltpu.SemaphoreType.DMA(())],
        input_output_aliases={2: 0},
        compiler_params=pltpu.CompilerParams(dimension_semantics=("arbitrary",), has_side_effects=True),
        name="moe_dispatch",
    )(pos.reshape(t // tt, 1, TOP_K * tt), x2, xs0)


def _expert_kernel(be_ref, nu_ref, xs_ref, wgu_ref, wd_ref, ys_ref):
    used = pl.program_id(0) < nu_ref[0]

    @pl.when(used)
    def _():
        xb = xs_ref[...].astype(BF16)
        gu = jnp.dot(xb, wgu_ref[0], preferred_element_type=F32)
        gate, up = gu[:, :D_EXPERT], gu[:, D_EXPERT:]
        hmid = (gate * jax.nn.sigmoid(gate) * up).astype(BF16)
        ys_ref[...] = jnp.dot(hmid, wd_ref[0], preferred_element_type=F32)

    @pl.when(jnp.logical_not(used))
    def _():
        ys_ref[...] = jnp.zeros_like(ys_ref)


def _experts(xs, block_e, n_used, w_gu, w_down):
    rows = xs.shape[0]
    n_blocks = rows // ROW_BLK
    blk = lambda b, be, nu: (jnp.minimum(b, nu[0] - 1), 0)
    out_blk = lambda b, be, nu: (b, 0)
    return pl.pallas_call(
        _expert_kernel,
        grid_spec=pltpu.PrefetchScalarGridSpec(
            num_scalar_prefetch=2,
            grid=(n_blocks,),
            in_specs=[
                pl.BlockSpec((ROW_BLK, D_MODEL), blk),
                pl.BlockSpec((1, D_MODEL, 2 * D_EXPERT), lambda b, be, nu: (be[jnp.minimum(b, nu[0] - 1)], 0, 0)),
                pl.BlockSpec((1, D_EXPERT, D_MODEL), lambda b, be, nu: (be[jnp.minimum(b, nu[0] - 1)], 0, 0)),
            ],
            out_specs=pl.BlockSpec((ROW_BLK, D_MODEL), out_blk),
        ),
        out_shape=jax.ShapeDtypeStruct((rows, D_MODEL), F32),
        compiler_params=_params(("arbitrary",)),
        name="moe_experts",
    )(block_e, n_used, xs, w_gu, w_down)


def _combine_kernel(pos_ref, wts_ref, x_ref, ys_hbm, lng_ref, lnb_ref, out_ref, outb_ref, ybuf, sem, *, tc):
    def start(t, c):
        for k in range(TOP_K):
            _row_copy(ys_hbm.at[pl.ds(pos_ref[0, 0, TOP_K * t + k], 1)], ybuf.at[k, pl.ds(t, 1)], sem).start()
        return c

    lax.fori_loop(0, tc, start, 0)

    def wait(t, c):
        for k in range(TOP_K):
            _row_copy(ys_hbm.at[pl.ds(0, 1)], ybuf.at[k, pl.ds(0, 1)], sem).wait()
        return c

    lax.fori_loop(0, tc, wait, 0)
    w = wts_ref[...]
    moe = w[:, 0:1] * ybuf[0] + w[:, 1:2] * ybuf[1]
    y = _layer_norm(ALPHA * x_ref[...] + moe, lng_ref[...], lnb_ref[...])
    out_ref[...] = y
    outb_ref[...] = y.astype(BF16)


def _combine(x2, wts, ys, pos, ln_g, ln_b, tc):
    t = x2.shape[0]
    assert t % tc == 0
    row = lambda w: pl.BlockSpec((tc, w), lambda i: (i, 0))
    return pl.pallas_call(
        functools.partial(_combine_kernel, tc=tc),
        grid=(t // tc,),
        in_specs=[pl.BlockSpec((1, 1, TOP_K * tc), lambda i: (i, 0, 0), memory_space=pltpu.SMEM),
                  row(LANES), row(D_MODEL),
                  pl.BlockSpec(memory_space=pl.ANY),
                  pl.BlockSpec((1, D_MODEL), lambda i: (0, 0)),
                  pl.BlockSpec((1, D_MODEL), lambda i: (0, 0))],
        out_specs=[row(D_MODEL), row(D_MODEL)],
        out_shape=[jax.ShapeDtypeStruct((t, D_MODEL), F32), jax.ShapeDtypeStruct((t, D_MODEL), BF16)],
        scratch_shapes=[pltpu.VMEM((TOP_K, tc, D_MODEL), F32), pltpu.SemaphoreType.DMA(())],
        compiler_params=_params(("arbitrary",)),
        name="moe_combine",
    )(pos.reshape(t // tc, 1, TOP_K * tc), wts, x2, ys, ln_g, ln_b)


def _moe_plan(ids, t):
    n_assign = t * TOP_K
    n_blocks = -(-n_assign // ROW_BLK) + N_EXPERTS
    e = ids[:, :TOP_K].reshape(-1)
    onehot = (e[:, None] == jnp.arange(N_EXPERTS, dtype=jnp.int32)[None, :]).astype(jnp.int32)
    csum = jnp.cumsum(onehot, axis=0)
    rank = jnp.sum(csum * onehot, axis=1) - 1
    counts = csum[-1]
    padded = ((counts + ROW_BLK - 1) // ROW_BLK) * ROW_BLK
    pad_end = jnp.cumsum(padded)
    pad_start = pad_end - padded
    pos = (jnp.sum(pad_start[None, :] * onehot, axis=1) + rank).astype(jnp.int32)
    blk_start = jnp.arange(n_blocks, dtype=jnp.int32) * ROW_BLK
    block_e = jnp.minimum(jnp.sum((pad_end[None, :] <= blk_start[:, None]).astype(jnp.int32), axis=1),
                          N_EXPERTS - 1).astype(jnp.int32)
    n_used = (pad_end[-1:] // ROW_BLK).astype(jnp.int32)
    return pos, block_e, n_used, n_blocks * ROW_BLK


def _rope_tables(seq):
    half = A_HEAD_DIM // 2
    inv = ROPE_THETA ** (-jnp.arange(half, dtype=F32) / half)
    ang = jnp.arange(seq).astype(F32)[:, None] * inv[None, :]
    cos, sin = jnp.cos(ang), jnp.sin(ang)
    reps = LANES // A_HEAD_DIM
    cos_t = jnp.tile(cos, (1, 2 * reps))
    sin_t = jnp.tile(jnp.concatenate([-sin, sin], axis=1), (1, reps))
    return cos_t, sin_t


def _prep_layer(l, w_in, b_in, gm_ln_g, gm_ln_b, w_sp, b_sp, w_a_out, w_b_out, w_o, ln1_g, ln1_b,
                w_xq, w_xkv, w_xo, ln2_g, ln2_b, w_rg, b_rg, w_re, b_re, w_gu, w_down, ln3_g, ln3_b):
    q0, k0, v0, u0, z0, g0 = 0, A_QKV, 2 * A_QKV, 3 * A_QKV, 3 * A_QKV + B_WIDTH, 3 * A_QKV + 2 * B_WIDTH
    order = [(u0, z0), (z0, g0), (g0, C_IN), (q0, k0), (k0, v0), (v0, u0)]
    w_in_p = jnp.concatenate([w_in[l][:, a:b] for a, b in order], axis=1).astype(BF16)
    b_in_p = jnp.concatenate([b_in[l][a:b] for a, b in order])[None, :]
    w_r = jnp.zeros((D_MODEL, LANES), F32)
    w_r = w_r.at[:, :N_EXPERT_GROUPS].set(w_rg[l]).at[:, N_EXPERT_GROUPS:N_EXPERT_GROUPS + N_EXPERTS].set(w_re[l])
    b_r = jnp.zeros((1, LANES), F32)
    b_r = b_r.at[0, :N_EXPERT_GROUPS].set(b_rg[l]).at[0, N_EXPERT_GROUPS:N_EXPERT_GROUPS + N_EXPERTS].set(b_re[l])
    w_r_hi = w_r.astype(BF16)
    head = jnp.arange(A_OUT, dtype=jnp.int32)[None, :] // A_HEAD_DIM
    head_bcast = (jnp.arange(LANES, dtype=jnp.int32)[:, None] == head).astype(BF16)
    row = lambda a: a[l][None, :]
    return dict(
        w_in=w_in_p, b_in=b_in_p,
        gm_ln_g=row(gm_ln_g), gm_ln_b=row(gm_ln_b),
        w_sp=w_sp[l].astype(BF16),
        b_sp_full=jnp.repeat(b_sp[l].T, B_WIDTH // B_GROUPS, axis=1),
        w_a_out=w_a_out[l].astype(BF16), w_b_out=w_b_out[l].astype(BF16), w_o=w_o[l].astype(BF16),
        head_bcast=head_bcast, ln1_g=row(ln1_g), ln1_b=row(ln1_b),
        w_xq=w_xq[l].astype(BF16), w_xkv=w_xkv[l].astype(BF16), w_xo=w_xo[l].astype(BF16),
        ln2_g=row(ln2_g), ln2_b=row(ln2_b),
        w_r_hi=w_r_hi, w_r_lo=(w_r - w_r_hi.astype(F32)).astype(BF16), b_r=b_r,
        w_gu=w_gu[l].astype(BF16), w_down=w_down[l].astype(BF16),
        ln3_g=row(ln3_g), ln3_b=row(ln3_b),
    )


def _trunk(x, mem, layers, tiles):
    bsz, seq, _ = x.shape
    t = bsz * seq
    tm = tiles["row"]
    x = x.reshape(t, D_MODEL)
    xb = x.astype(BF16)
    memf = mem.reshape(bsz * N_MEM, D_MODEL)
    cos_t, sin_t = _rope_tables(seq)
    for lw in layers:
        htok = _inproj_tok(xb, lw["w_in"], lw["b_in"], tiles["inproj_tok"])
        outs, lses = [], []
        for g, (_, dil) in enumerate(A_PATTERNS):
            qkv = _inproj_qkv(xb, lw["w_in"], lw["b_in"], cos_t, sin_t, seq, g, dil, tm)
            o, lse = _dilated_attention(qkv, bsz, seq, dil, tm)
            outs.append(o)
            lses.append(lse)
        x1 = _mix(htok, outs, lses, x, lw, tiles["mix"], tm)
        kv = _matmul(memf, lw["w_xkv"], N_MEM, D_MODEL, BF16)
        x2, ids, wts = _xattn(x1, kv, lw, bsz, seq, tiles["xattn"])
        pos, block_e, n_used, rows = _moe_plan(ids, t)
        xs = _dispatch(x2, pos, rows, tiles["dispatch"])
        ys = _experts(xs, block_e, n_used, lw["w_gu"], lw["w_down"])
        x, xb = _combine(x2, wts, ys, pos, lw["ln3_g"], lw["ln3_b"], tiles["combine"])
    return x.reshape(bsz, seq, D_MODEL)


TILES = dict(row=ROW_TILE, inproj_tok=1024, mix=256, xattn=512, dispatch=512, combine=256)


def kernel(x_prompt, x_sample, mem_prompt, mem_sample, w_in, b_in, gm_ln_g, gm_ln_b, w_sp, b_sp, w_a_out, w_b_out, w_o, ln1_g, ln1_b, w_xq, w_xkv, w_xo, ln2_g, ln2_b, w_rg, b_rg, w_re, b_re, w_gu, w_down, ln3_g, ln3_b):
    params = (w_in, b_in, gm_ln_g, gm_ln_b, w_sp, b_sp, w_a_out, w_b_out, w_o, ln1_g, ln1_b,
              w_xq, w_xkv, w_xo, ln2_g, ln2_b, w_rg, b_rg, w_re, b_re, w_gu, w_down, ln3_g, ln3_b)
    layers = [_prep_layer(l, *params) for l in range(DEPTH)]
    y_prompt = _trunk(x_prompt, mem_prompt, layers, TILES)
    y_sample = _trunk(x_sample, mem_sample, layers, TILES)
    return (y_prompt, y_sample)
```
